```python
import jax, jax.numpy as jnp
from jax import lax
import numpy as np

D_MODEL = 1024
BATCH = 16
SEQ = 4096
DEPTH = 2

CHUNK = 64
SGU_BLOCK = 128
SGU_HEADS = 4
SGU_DIM = D_MODEL // 2
SGU_HEAD_DIM = SGU_DIM // SGU_HEADS
POOL_WINDOWS = (2, 4, 8, 16)
POOL_GROUPS = len(POOL_WINDOWS)
POOL_DIM = D_MODEL // 2
POOL_GROUP_DIM = POOL_DIM // POOL_GROUPS
IN_AB = 2 * SGU_DIM + POOL_DIM
MIX_AB = SGU_DIM + POOL_DIM
CONV_WIDTH = 3
CONV_DIM = D_MODEL
D_FF = ((-(-8 * D_MODEL // 3) + 255) // 256) * 256
N_EVEN = (DEPTH + 1) // 2
N_ODD = DEPTH // 2
EPS = 1e-6

kernel_name = "hybrid_sgu_pool_shortconv_trunk"


def rms_norm(x, g):
    xf = x.astype(jnp.float32)
    y = xf * lax.rsqrt(jnp.mean(xf * xf, axis=-1, keepdims=True) + EPS)
    return (y * g.astype(jnp.float32)).astype(x.dtype)


def layer_norm(x, g, b):
    xf = x.astype(jnp.float32)
    mu = jnp.mean(xf, axis=-1, keepdims=True)
    xc = xf - mu
    var = jnp.mean(xc * xc, axis=-1, keepdims=True)
    y = xc * lax.rsqrt(var + EPS)
    return (y * g.astype(jnp.float32) + b.astype(jnp.float32)).astype(x.dtype)


def sgu_mixer(z, ln_g, ln_b, ws, bs):
    bsz, s, _ = z.shape
    u = z[..., :SGU_DIM]
    v = layer_norm(z[..., SGU_DIM:], ln_g, ln_b)
    v = v.reshape(bsz, s // SGU_BLOCK, SGU_BLOCK, SGU_HEADS, SGU_HEAD_DIM)
    chunk_id = jnp.arange(SGU_BLOCK) // CHUNK
    mask = chunk_id[None, :] <= chunk_id[:, None]
    w = jnp.where(mask[None], ws, jnp.zeros_like(ws))
    vs = jnp.einsum('hij,bnjhd->bnihd', w, v) + bs[None, None, :, :, None]
    return u * vs.reshape(bsz, s, SGU_DIM)


def pool_mixer(p, pool_w, pool_b, pool_scale):
    s = p.shape[1]
    pf = p.astype(jnp.float32)
    cs = jnp.cumsum(pf, axis=1)
    t = jnp.arange(s)
    outs = []
    for g, win in enumerate(POOL_WINDOWS):
        sl = slice(g * POOL_GROUP_DIM, (g + 1) * POOL_GROUP_DIM)
        c = cs[..., sl]
        c_prev = jnp.pad(c[:, :-win], ((0, 0), (win, 0), (0, 0)))
        count = jnp.minimum(t + 1, win).astype(jnp.float32)[None, :, None]
        d = ((c - c_prev) / count - pf[..., sl]).astype(p.dtype)
        outs.append(d @ pool_w[g] + pool_b[g])
    return jnp.concatenate(outs, axis=-1) * pool_scale


def short_conv_mixer(h, conv_w, conv_b):
    s = h.shape[1]
    b_gate = h[..., :CONV_DIM]
    c_gate = h[..., CONV_DIM:2 * CONV_DIM]
    hv = h[..., 2 * CONV_DIM:]
    q = c_gate * hv
    qp = jnp.pad(q, ((0, 0), (CONV_WIDTH - 1, 0), (0, 0)))
    y = conv_b + sum(conv_w[k] * qp[:, k:k + s] for k in range(CONV_WIDTH))
    return b_gate * y


def swiglu(x, w_gate, w_up, w_down):
    return (jax.nn.silu(x @ w_gate) * (x @ w_up)) @ w_down


def setup_inputs(seed: int = 0) -> dict:
    key = jax.random.key(seed)
    ks = jax.random.split(key, 32)
    f32 = jnp.float32

    def nrm(k, shape, fan_in):
        return jax.random.normal(k, shape, f32) * (fan_in ** -0.5)

    def gain(k, shape):
        return 1.0 + 0.05 * jax.random.normal(k, shape, f32)

    def small(k, shape):
        return 0.02 * jax.random.normal(k, shape, f32)

    return {
        "x": jax.random.normal(ks[0], (BATCH, SEQ, D_MODEL), f32),
        "even_norm": gain(ks[1], (N_EVEN, D_MODEL)),
        "even_w_in": nrm(ks[2], (N_EVEN, D_MODEL, IN_AB), D_MODEL),
        "even_sgu_ln_g": gain(ks[3], (N_EVEN, SGU_DIM)),
        "even_sgu_ln_b": small(ks[4], (N_EVEN, SGU_DIM)),
        "even_sgu_ws": nrm(ks[5], (N_EVEN, SGU_HEADS, SGU_BLOCK, SGU_BLOCK), SGU_BLOCK),
        "even_sgu_bs": 1.0 + 0.1 * jax.random.normal(ks[6], (N_EVEN, SGU_BLOCK, SGU_HEADS), f32),
        "even_pool_w": nrm(ks[7], (N_EVEN, POOL_GROUPS, POOL_GROUP_DIM, POOL_GROUP_DIM), POOL_GROUP_DIM),
        "even_pool_b": small(ks[8], (N_EVEN, POOL_GROUPS, POOL_GROUP_DIM)),
        "even_pool_scale": gain(ks[9], (N_EVEN, POOL_DIM)),
        "even_w_out": nrm(ks[10], (N_EVEN, MIX_AB, D_MODEL), MIX_AB),
        "odd_norm": gain(ks[11], (N_ODD, D_MODEL)),
        "odd_w_in": nrm(ks[12], (N_ODD, D_MODEL, 3 * CONV_DIM), D_MODEL),
        "odd_conv_w": nrm(ks[13], (N_ODD, CONV_WIDTH, CONV_DIM), CONV_WIDTH),
        "odd_conv_b": small(ks[14], (N_ODD, CONV_DIM)),
        "odd_w_out": nrm(ks[15], (N_ODD, CONV_DIM, D_MODEL), CONV_DIM),
        "ffn_norm": gain(ks[16], (DEPTH, D_MODEL)),
        "ffn_w_gate": nrm(ks[17], (DEPTH, D_MODEL, D_FF), D_MODEL),
        "ffn_w_up": nrm(ks[18], (DEPTH, D_MODEL, D_FF), D_MODEL),
        "ffn_w_down": nrm(ks[19], (DEPTH, D_FF, D_MODEL), D_FF),
        "final_norm": gain(ks[20], (D_MODEL,)),
    }


def reference(x, even_norm, even_w_in, even_sgu_ln_g, even_sgu_ln_b, even_sgu_ws,
              even_sgu_bs, even_pool_w, even_pool_b, even_pool_scale, even_w_out,
              odd_norm, odd_w_in, odd_conv_w, odd_conv_b, odd_w_out,
              ffn_norm, ffn_w_gate, ffn_w_up, ffn_w_down, final_norm):
    for layer in range(DEPTH):
        i = layer // 2
        if layer % 2 == 0:
            hn = rms_norm(x, even_norm[i])
            h = hn @ even_w_in[i]
            z = jax.nn.gelu(h[..., :2 * SGU_DIM], approximate=False)
            a_out = sgu_mixer(z, even_sgu_ln_g[i], even_sgu_ln_b[i],
                              even_sgu_ws[i], even_sgu_bs[i])
            b_out = pool_mixer(h[..., 2 * SGU_DIM:], even_pool_w[i],
                               even_pool_b[i], even_pool_scale[i])
            mix = jnp.concatenate([a_out, b_out], axis=-1) @ even_w_out[i]
        else:
            hn = rms_norm(x, odd_norm[i])
            h = hn @ odd_w_in[i]
            mix = short_conv_mixer(h, odd_conv_w[i], odd_conv_b[i]) @ odd_w_out[i]
        x = x + mix
        hn = rms_norm(x, ffn_norm[layer])
        x = x + swiglu(hn, ffn_w_gate[layer], ffn_w_up[layer], ffn_w_down[layer])
    return rms_norm(x, final_norm)
```

```python
import functools
import math

import jax
import jax.numpy as jnp
from jax import lax
from jax.experimental import pallas as pl
from jax.experimental.pallas import tpu as pltpu

D_MODEL = 1024
SEQ = 4096
CHUNK = 64
SGU_BLOCK = 128
SGU_HEADS = 4
SGU_DIM = D_MODEL // 2
SGU_HEAD_DIM = SGU_DIM // SGU_HEADS
POOL_WINDOWS = (2, 4, 8, 16)
POOL_DIM = D_MODEL // 2
POOL_GROUP_DIM = POOL_DIM // len(POOL_WINDOWS)
POOL_HALO = 16
CONV_WIDTH = 3
CONV_HALO = 8
D_FF = 2816
EPS = 1e-6

TOKEN_BLOCK = 512
FF_CHUNKS = ((0, 768), (768, 768), (1536, 768), (2304, 512))
V7X_VMEM_LIMIT_BYTES = 56 * 1024 * 1024

BF16 = jnp.bfloat16
F32 = jnp.float32


def _rms_norm(x, g):
    ms = jnp.mean(x * x, axis=-1, keepdims=True)
    return x * lax.rsqrt(ms + EPS) * g


def _dot(a, b):
    return jnp.dot(a, b, preferred_element_type=F32)


def _gelu_exact(x):
    return 0.5 * x * (1.0 + lax.erf(x * math.sqrt(0.5)))


def _row_shift(x, k):
    return pltpu.roll(x, k, 0)


def _mixer_even_kernel(x_ref, norm_ref, w_in_ref, ln_g_ref, ln_b_ref, ws_ref, bs_ref,
                       pool_w_ref, pool_b_ref, pool_scale_ref, w_out_ref,
                       o_ref, tail_ref, *, blocks_per_seq):
    tm = x_ref.shape[0]
    seq_block = pl.program_id(0) % blocks_per_seq

    @pl.when(seq_block == 0)
    def _():
        tail_ref[...] = jnp.zeros_like(tail_ref)

    x = x_ref[...]
    hn = _rms_norm(x, norm_ref[...]).astype(BF16)
    h = _dot(hn, w_in_ref[...])

    u = _gelu_exact(h[:, :SGU_DIM])
    zv = _gelu_exact(h[:, SGU_DIM:2 * SGU_DIM])
    mu = jnp.mean(zv, axis=-1, keepdims=True)
    zc = zv - mu
    var = jnp.mean(zc * zc, axis=-1, keepdims=True)
    v = (zc * lax.rsqrt(var + EPS) * ln_g_ref[...] + ln_b_ref[...]).astype(BF16)

    row = lax.broadcasted_iota(jnp.int32, (SGU_BLOCK, SGU_BLOCK), 0) // CHUNK
    col = lax.broadcasted_iota(jnp.int32, (SGU_BLOCK, SGU_BLOCK), 1) // CHUNK
    causal = col <= row
    bs = bs_ref[...]
    vs_heads = []
    for hd in range(SGU_HEADS):
        w = jnp.where(causal, ws_ref[hd], jnp.zeros((), BF16))
        cols = slice(hd * SGU_HEAD_DIM, (hd + 1) * SGU_HEAD_DIM)
        blocks = [
            _dot(w, v[n * SGU_BLOCK:(n + 1) * SGU_BLOCK, cols]) + bs[:, cols]
            for n in range(tm // SGU_BLOCK)
        ]
        vs_heads.append(jnp.concatenate(blocks, axis=0))
    a_out = (u * jnp.concatenate(vs_heads, axis=1)).astype(BF16)

    p = h[:, 2 * SGU_DIM:]
    pext = jnp.concatenate([tail_ref[...], p], axis=0)
    tail_ref[...] = p[tm - POOL_HALO:, :]

    t1 = (lax.broadcasted_iota(jnp.int32, (tm, POOL_GROUP_DIM), 0)
          + (seq_block * tm + 1)).astype(F32)
    inv_t1 = 1.0 / t1
    pooled = []
    for g, win in enumerate(POOL_WINDOWS):
        cols = slice(g * POOL_GROUP_DIM, (g + 1) * POOL_GROUP_DIM)
        s = pext[:, cols]
        span = 1
        while span < win:
            s = s + _row_shift(s, span)
            span *= 2
        s = s[POOL_HALO:, :]
        inv_count = jnp.where(t1 < win, inv_t1, 1.0 / win)
        d = (s * inv_count - p[:, cols]).astype(BF16)
        pooled.append((_dot(d, pool_w_ref[g]) + pool_b_ref[:, cols]) * pool_scale_ref[:, cols])
    b_out = jnp.concatenate(pooled, axis=1).astype(BF16)

    mix = _dot(a_out, w_out_ref[:SGU_DIM, :]) + _dot(b_out, w_out_ref[SGU_DIM:, :])
    o_ref[...] = x + mix


def _mixer_odd_kernel(x_ref, norm_ref, w_in_ref, conv_w_ref, conv_b_ref, w_out_ref,
                      o_ref, tail_ref, *, blocks_per_seq):
    tm = x_ref.shape[0]
    seq_block = pl.program_id(0) % blocks_per_seq

    @pl.when(seq_block == 0)
    def _():
        tail_ref[...] = jnp.zeros_like(tail_ref)

    x = x_ref[...]
    hn = _rms_norm(x, norm_ref[...]).astype(BF16)
    b_gate = _dot(hn, w_in_ref[:, :D_MODEL])
    q = _dot(hn, w_in_ref[:, D_MODEL:2 * D_MODEL]) * _dot(hn, w_in_ref[:, 2 * D_MODEL:])

    qext = jnp.concatenate([tail_ref[...], q], axis=0)
    tail_ref[...] = q[tm - CONV_HALO:, :]
    y = conv_b_ref[...] + conv_w_ref[CONV_WIDTH - 1:CONV_WIDTH, :] * q
    for k in range(CONV_WIDTH - 1):
        shift = CONV_WIDTH - 1 - k
        y = y + conv_w_ref[k:k + 1, :] * _row_shift(qext, shift)[CONV_HALO:, :]
    o_ref[...] = x + _dot((b_gate * y).astype(BF16), w_out_ref[...])


def _ffn_kernel(x_ref, norm_ref, w_gate_ref, w_up_ref, w_down_ref, *rest, final_norm):
    if final_norm:
        final_ref, o_ref = rest
    else:
        (o_ref,) = rest
    x = x_ref[...]
    hn = _rms_norm(x, norm_ref[...]).astype(BF16)
    acc = x
    for start, size in FF_CHUNKS:
        gate = _dot(hn, w_gate_ref[:, start:start + size])
        up = _dot(hn, w_up_ref[:, start:start + size])
        act = (gate * (1.0 / (1.0 + jnp.exp(-gate))) * up).astype(BF16)
        acc = acc + _dot(act, w_down_ref[start:start + size, :])
    if final_norm:
        acc = _rms_norm(acc, final_ref[...])
    o_ref[...] = acc


def _resident(shape):
    nd = len(shape)
    return pl.BlockSpec(shape, lambda i: (0,) * nd, pipeline_mode=pl.Buffered(1))


def _token_call(body, name, x, params, scratch_shapes=()):
    tokens, d = x.shape
    tok_spec = pl.BlockSpec((TOKEN_BLOCK, d), lambda i: (i, 0))
    return pl.pallas_call(
        body,
        out_shape=jax.ShapeDtypeStruct((tokens, d), F32),
        grid=(tokens // TOKEN_BLOCK,),
        in_specs=[tok_spec] + [_resident(p.shape) for p in params],
        out_specs=tok_spec,
        scratch_shapes=list(scratch_shapes),
        compiler_params=pltpu.CompilerParams(
            dimension_semantics=("arbitrary",),
            vmem_limit_bytes=V7X_VMEM_LIMIT_BYTES,
        ),
        name=name,
    )(x, *params)


def _row(v):
    return v.reshape(1, -1).astype(F32)


def kernel(x, even_norm, even_w_in, even_sgu_ln_g, even_sgu_ln_b, even_sgu_ws, even_sgu_bs, even_pool_w, even_pool_b, even_pool_scale, even_w_out, odd_norm, odd_w_in, odd_conv_w, odd_conv_b, odd_w_out, ffn_norm, ffn_w_gate, ffn_w_up, ffn_w_down, final_norm):
    batch, seq, d = x.shape
    assert (seq, d) == (SEQ, D_MODEL) and SEQ % TOKEN_BLOCK == 0 and TOKEN_BLOCK % SGU_BLOCK == 0
    depth = ffn_norm.shape[0]
    blocks_per_seq = seq // TOKEN_BLOCK
    xt = x.reshape(batch * seq, d)

    for layer in range(depth):
        i = layer // 2
        if layer % 2 == 0:
            bs_full = jnp.repeat(even_sgu_bs[i].astype(F32), SGU_HEAD_DIM, axis=1)
            params = (
                _row(even_norm[i]), even_w_in[i].astype(BF16),
                _row(even_sgu_ln_g[i]), _row(even_sgu_ln_b[i]),
                even_sgu_ws[i].astype(BF16), bs_full,
                even_pool_w[i].astype(BF16), _row(even_pool_b[i]), _row(even_pool_scale[i]),
                even_w_out[i].astype(BF16),
            )
            xt = _token_call(
                functools.partial(_mixer_even_kernel, blocks_per_seq=blocks_per_seq),
                f"mixer_even_{i}", xt, params,
                scratch_shapes=[pltpu.VMEM((POOL_HALO, POOL_DIM), F32)])
        else:
            params = (
                _row(odd_norm[i]), odd_w_in[i].astype(BF16),
                odd_conv_w[i].astype(F32), _row(odd_conv_b[i]), odd_w_out[i].astype(BF16),
            )
            xt = _token_call(
                functools.partial(_mixer_odd_kernel, blocks_per_seq=blocks_per_seq),
                f"mixer_odd_{i}", xt, params,
                scratch_shapes=[pltpu.VMEM((CONV_HALO, D_MODEL), F32)])

        last = layer == depth - 1
        params = (_row(ffn_norm[layer]), ffn_w_gate[layer].astype(BF16),
                  ffn_w_up[layer].astype(BF16), ffn_w_down[layer].astype(BF16))
        if last:
            params = params + (_row(final_norm),)
        xt = _token_call(functools.partial(_ffn_kernel, final_norm=last),
                         f"ffn_{layer}", xt, params)

    return xt.reshape(batch, seq, d)
```

```python
import functools
import math

import jax
import jax.numpy as jnp
from jax import lax
from jax.experimental import pallas as pl
from jax.experimental.pallas import tpu as pltpu

D_MODEL = 1024
SEQ = 4096
CHUNK = 64
SGU_BLOCK = 128
SGU_HEADS = 4
SGU_DIM = D_MODEL // 2
SGU_HEAD_DIM = SGU_DIM // SGU_HEADS
POOL_WINDOWS = (2, 4, 8, 16)
POOL_DIM = D_MODEL // 2
POOL_GROUP_DIM = POOL_DIM // len(POOL_WINDOWS)
POOL_PAIR_DIM = 2 * POOL_GROUP_DIM
POOL_HALO = 16
CONV_WIDTH = 3
CONV_HALO = 8
D_FF = 2816
EPS = 1e-6

TOKEN_BLOCK = 2048
ROW_CHAIN = 512
FF_CHUNKS = ((0, 768), (768, 768), (1536, 768), (2304, 512))
V7X_VMEM_LIMIT_BYTES = 56 * 1024 * 1024

BF16 = jnp.bfloat16
F32 = jnp.float32


def _rms_norm(x, g):
    ms = jnp.mean(x * x, axis=-1, keepdims=True)
    return x * lax.rsqrt(ms + EPS) * g


def _dot(a, b):
    return jnp.dot(a, b, preferred_element_type=F32)


def _gelu_exact(x):
    return 0.5 * x * (1.0 + lax.erf(x * math.sqrt(0.5)))


def _row_shift(x, k):
    return pltpu.roll(x, k, 0)


def _sequence_block():
    return pl.program_id(0) % (SEQ // TOKEN_BLOCK)


def _mixer_even_kernel(x_ref, norm_ref, w_in_ref, ln_g_ref, ln_b_ref, ws_ref, bs_ref,
                       pool_w_ref, pool_b_ref, pool_scale_ref, w_out_ref,
                       o_ref, tail_ref):
    seq_block = _sequence_block()

    @pl.when(seq_block == 0)
    def _():
        tail_ref[...] = jnp.zeros_like(tail_ref)

    row = lax.broadcasted_iota(jnp.int32, (SGU_BLOCK, SGU_BLOCK), 0) // CHUNK
    col = lax.broadcasted_iota(jnp.int32, (SGU_BLOCK, SGU_BLOCK), 1) // CHUNK
    w_heads = [jnp.where(col <= row, ws_ref[hd], jnp.zeros((), BF16)) for hd in range(SGU_HEADS)]
    n_blocks = ROW_CHAIN // SGU_BLOCK

    def in_proj(r):
        hn = _rms_norm(x_ref[r:r + ROW_CHAIN, :], norm_ref[...]).astype(BF16)
        return _dot(hn, w_in_ref[...])

    chains = range(0, TOKEN_BLOCK, ROW_CHAIN)
    projected = [in_proj(r) for r in chains]
    tail = tail_ref[...]
    for r, h in zip(chains, projected):

        u = _gelu_exact(h[:, :SGU_DIM])
        zv = _gelu_exact(h[:, SGU_DIM:2 * SGU_DIM])
        mu = jnp.mean(zv, axis=-1, keepdims=True)
        zc = zv - mu
        var = jnp.mean(zc * zc, axis=-1, keepdims=True)
        v = (zc * lax.rsqrt(var + EPS) * ln_g_ref[...] + ln_b_ref[...]).astype(BF16)

        mixed = []
        for hd in range(SGU_HEADS):
            cols = slice(hd * SGU_HEAD_DIM, (hd + 1) * SGU_HEAD_DIM)
            v_blocks = jnp.concatenate(
                [v[n * SGU_BLOCK:(n + 1) * SGU_BLOCK, cols] for n in range(n_blocks)], axis=1)
            mixed.append(_dot(w_heads[hd], v_blocks))
        vs = jnp.concatenate(
            [jnp.concatenate([m[:, n * SGU_HEAD_DIM:(n + 1) * SGU_HEAD_DIM] for m in mixed], axis=1)
             + bs_ref[...] for n in range(n_blocks)], axis=0)
        a_out = (u * vs).astype(BF16)

        p = h[:, 2 * SGU_DIM:]
        pext = jnp.concatenate([tail, p], axis=0)
        tail = p[ROW_CHAIN - POOL_HALO:, :]
        t1 = (lax.broadcasted_iota(jnp.int32, (ROW_CHAIN, POOL_GROUP_DIM), 0)
              + (seq_block * TOKEN_BLOCK + r + 1)).astype(F32)
        inv_t1 = 1.0 / t1
        diffs = []
        for g, win in enumerate(POOL_WINDOWS):
            cols = slice(g * POOL_GROUP_DIM, (g + 1) * POOL_GROUP_DIM)
            s = pext[:, cols]
            span = 1
            while span < win:
                s = s + _row_shift(s, span)
                span *= 2
            inv_count = jnp.where(t1 < win, inv_t1, 1.0 / win)
            diffs.append((s[POOL_HALO:, :] * inv_count - p[:, cols]).astype(BF16))
        pooled = []
        for pair in range(len(POOL_WINDOWS) // 2):
            cols = slice(pair * POOL_PAIR_DIM, (pair + 1) * POOL_PAIR_DIM)
            d_pair = jnp.concatenate(diffs[2 * pair:2 * pair + 2], axis=1)
            pooled.append((_dot(d_pair, pool_w_ref[pair]) + pool_b_ref[:, cols])
                          * pool_scale_ref[:, cols])
        b_out = jnp.concatenate(pooled, axis=1).astype(BF16)

        mix = _dot(jnp.concatenate([a_out, b_out], axis=1), w_out_ref[...])
        o_ref[r:r + ROW_CHAIN, :] = x_ref[r:r + ROW_CHAIN, :] + mix
    tail_ref[...] = tail


def _mixer_odd_kernel(x_ref, norm_ref, w_in_ref, conv_w_ref, conv_b_ref, w_out_ref,
                      o_ref, tail_ref):
    @pl.when(_sequence_block() == 0)
    def _():
        tail_ref[...] = jnp.zeros_like(tail_ref)

    def in_proj(r):
        hn = _rms_norm(x_ref[r:r + ROW_CHAIN, :], norm_ref[...]).astype(BF16)
        b_gate = _dot(hn, w_in_ref[:, :D_MODEL])
        q = _dot(hn, w_in_ref[:, D_MODEL:2 * D_MODEL]) * _dot(hn, w_in_ref[:, 2 * D_MODEL:])
        return b_gate, q

    chains = range(0, TOKEN_BLOCK, ROW_CHAIN)
    projected = [in_proj(r) for r in chains]
    tail = tail_ref[...]
    for r, (b_gate, q) in zip(chains, projected):
        qext = jnp.concatenate([tail, q], axis=0)
        tail = q[ROW_CHAIN - CONV_HALO:, :]
        y = conv_b_ref[...] + conv_w_ref[CONV_WIDTH - 1:CONV_WIDTH, :] * q
        for k in range(CONV_WIDTH - 1):
            shift = CONV_WIDTH - 1 - k
            y = y + conv_w_ref[k:k + 1, :] * _row_shift(qext, shift)[CONV_HALO:, :]
        o_ref[r:r + ROW_CHAIN, :] = (x_ref[r:r + ROW_CHAIN, :]
                                     + _dot((b_gate * y).astype(BF16), w_out_ref[...]))
    tail_ref[...] = tail


def _ffn_kernel(x_ref, norm_ref, w_gate_ref, w_up_ref, w_down_ref, *rest, final_norm):
    if final_norm:
        final_ref, o_ref = rest
    else:
        (o_ref,) = rest
    for r in range(0, TOKEN_BLOCK, ROW_CHAIN):
        x = x_ref[r:r + ROW_CHAIN, :]
        hn = _rms_norm(x, norm_ref[...]).astype(BF16)
        acc = x
        for start, size in FF_CHUNKS:
            gate = _dot(hn, w_gate_ref[:, start:start + size])
            up = _dot(hn, w_up_ref[:, start:start + size])
            act = (gate * (1.0 / (1.0 + jnp.exp(-gate))) * up).astype(BF16)
            acc = acc + _dot(act, w_down_ref[start:start + size, :])
        if final_norm:
            acc = _rms_norm(acc, final_ref[...])
        o_ref[r:r + ROW_CHAIN, :] = acc


def _resident(shape):
    nd = len(shape)
    return pl.BlockSpec(shape, lambda i: (0,) * nd, pipeline_mode=pl.Buffered(1))


def _token_call(body, name, x, params, scratch_shapes=()):
    tokens, d = x.shape
    tok_spec = pl.BlockSpec((TOKEN_BLOCK, d), lambda i: (i, 0))
    return pl.pallas_call(
        body,
        out_shape=jax.ShapeDtypeStruct((tokens, d), F32),
        grid=(tokens // TOKEN_BLOCK,),
        in_specs=[tok_spec] + [_resident(p.shape) for p in params],
        out_specs=tok_spec,
        scratch_shapes=list(scratch_shapes),
        compiler_params=pltpu.CompilerParams(
            dimension_semantics=("arbitrary",),
            vmem_limit_bytes=V7X_VMEM_LIMIT_BYTES,
        ),
        name=name,
    )(x, *params)


def _row(v):
    return v.reshape(1, -1).astype(F32)


def _pair_block_diagonal(w):
    g, c, _ = w.shape
    zero = jnp.zeros((g // 2, c, c), w.dtype)
    top = jnp.concatenate([w[0::2], zero], axis=2)
    bottom = jnp.concatenate([zero, w[1::2]], axis=2)
    return jnp.concatenate([top, bottom], axis=1)


def kernel(x, even_norm, even_w_in, even_sgu_ln_g, even_sgu_ln_b, even_sgu_ws, even_sgu_bs, even_pool_w, even_pool_b, even_pool_scale, even_w_out, odd_norm, odd_w_in, odd_conv_w, odd_conv_b, odd_w_out, ffn_norm, ffn_w_gate, ffn_w_up, ffn_w_down, final_norm):
    batch, seq, d = x.shape
    assert (seq, d) == (SEQ, D_MODEL)
    assert SEQ % TOKEN_BLOCK == 0 and TOKEN_BLOCK % ROW_CHAIN == 0 and ROW_CHAIN % SGU_BLOCK == 0
    depth = ffn_norm.shape[0]
    xt = x.reshape(batch * seq, d)

    for layer in range(depth):
        i = layer // 2
        if layer % 2 == 0:
            bs_full = jnp.repeat(even_sgu_bs[i].astype(F32), SGU_HEAD_DIM, axis=1)
            params = (
                _row(even_norm[i]), even_w_in[i].astype(BF16),
                _row(even_sgu_ln_g[i]), _row(even_sgu_ln_b[i]),
                even_sgu_ws[i].astype(BF16), bs_full,
                _pair_block_diagonal(even_pool_w[i].astype(BF16)),
                _row(even_pool_b[i]), _row(even_pool_scale[i]),
                even_w_out[i].astype(BF16),
            )
            xt = _token_call(_mixer_even_kernel, f"mixer_even_{i}", xt, params,
                             scratch_shapes=[pltpu.VMEM((POOL_HALO, POOL_DIM), F32)])
        else:
            params = (
                _row(odd_norm[i]), odd_w_in[i].astype(BF16),
                odd_conv_w[i].astype(F32), _row(odd_conv_b[i]), odd_w_out[i].astype(BF16),
            )
            xt = _token_call(_mixer_odd_kernel, f"mixer_odd_{i}", xt, params,
                             scratch_shapes=[pltpu.VMEM((CONV_HALO, D_MODEL), F32)])

        last = layer == depth - 1
        params = (_row(ffn_norm[layer]), ffn_w_gate[layer].astype(BF16),
                  ffn_w_up[layer].astype(BF16), ffn_w_down[layer].astype(BF16))
        if last:
            params = params + (_row(final_norm),)
        xt = _token_call(functools.partial(_ffn_kernel, final_norm=last),
                         f"ffn_{layer}", xt, params)

    return xt.reshape(batch, seq, d)
```

```python
import functools
import math

import jax
import jax.numpy as jnp
from jax import lax
from jax.experimental import pallas as pl
from jax.experimental.pallas import tpu as pltpu

D_MODEL = 1024
SEQ = 4096
CHUNK = 64
SGU_BLOCK = 128
SGU_HEADS = 4
SGU_DIM = D_MODEL // 2
SGU_HEAD_DIM = SGU_DIM // SGU_HEADS
POOL_WINDOWS = (2, 4, 8, 16)
POOL_DIM = D_MODEL // 2
POOL_GROUP_DIM = POOL_DIM // len(POOL_WINDOWS)
POOL_PAIR_DIM = 2 * POOL_GROUP_DIM
POOL_HALO = 16
CONV_WIDTH = 3
CONV_HALO = 8
D_FF = 2816
EPS = 1e-6

MIXER_TILING = (2048, 1024)
FFN_TILING = (1024, 1024)
FF_CHUNKS = ((0, 768), (768, 768), (1536, 768), (2304, 512))
V7X_VMEM_LIMIT_BYTES = 56 * 1024 * 1024

BF16 = jnp.bfloat16
F32 = jnp.float32


def _rms_norm(x, g):
    ms = jnp.mean(x * x, axis=-1, keepdims=True)
    return x * lax.rsqrt(ms + EPS) * g


def _dot(a, b):
    return jnp.dot(a, b, preferred_element_type=F32)


def _gelu_exact(x):
    return 0.5 * x * (1.0 + lax.erf(x * math.sqrt(0.5)))


def _row_shift(x, k):
    return pltpu.roll(x, k, 0)


def _sequence_block(token_block):
    return pl.program_id(0) % (SEQ // token_block)


def _mixer_even_kernel(x_ref, norm_ref, w_in_ref, ln_g_ref, ln_b_ref, ws_ref, bs_ref,
                       pool_w_ref, pool_b_ref, pool_scale_ref, w_out_ref,
                       o_ref, tail_ref, *, chain_rows):
    token_block = x_ref.shape[0]
    seq_block = _sequence_block(token_block)

    @pl.when(seq_block == 0)
    def _():
        tail_ref[...] = jnp.zeros_like(tail_ref)

    row = lax.broadcasted_iota(jnp.int32, (SGU_BLOCK, SGU_BLOCK), 0) // CHUNK
    col = lax.broadcasted_iota(jnp.int32, (SGU_BLOCK, SGU_BLOCK), 1) // CHUNK
    w_heads = [jnp.where(col <= row, ws_ref[hd], jnp.zeros((), BF16)) for hd in range(SGU_HEADS)]
    n_blocks = chain_rows // SGU_BLOCK

    def in_proj(r):
        hn = _rms_norm(x_ref[r:r + chain_rows, :], norm_ref[...]).astype(BF16)
        return _dot(hn, w_in_ref[...])

    chains = range(0, token_block, chain_rows)
    projected = [in_proj(r) for r in chains]
    tail = tail_ref[...]
    for r, h in zip(chains, projected):

        u = _gelu_exact(h[:, :SGU_DIM])
        zv = _gelu_exact(h[:, SGU_DIM:2 * SGU_DIM])
        mu = jnp.mean(zv, axis=-1, keepdims=True)
        zc = zv - mu
        var = jnp.mean(zc * zc, axis=-1, keepdims=True)
        v = (zc * lax.rsqrt(var + EPS) * ln_g_ref[...] + ln_b_ref[...]).astype(BF16)

        mixed = []
        for hd in range(SGU_HEADS):
            cols = slice(hd * SGU_HEAD_DIM, (hd + 1) * SGU_HEAD_DIM)
            v_blocks = jnp.concatenate(
                [v[n * SGU_BLOCK:(n + 1) * SGU_BLOCK, cols] for n in range(n_blocks)], axis=1)
            mixed.append(_dot(w_heads[hd], v_blocks))
        vs = jnp.concatenate(
            [jnp.concatenate([m[:, n * SGU_HEAD_DIM:(n + 1) * SGU_HEAD_DIM] for m in mixed], axis=1)
             + bs_ref[...] for n in range(n_blocks)], axis=0)
        a_out = (u * vs).astype(BF16)

        p = h[:, 2 * SGU_DIM:]
        pext = jnp.concatenate([tail, p], axis=0)
        tail = p[chain_rows - POOL_HALO:, :]
        t1 = (lax.broadcasted_iota(jnp.int32, (chain_rows, POOL_GROUP_DIM), 0)
              + (seq_block * token_block + r + 1)).astype(F32)
        inv_t1 = 1.0 / t1
        diffs = []
        for g, win in enumerate(POOL_WINDOWS):
            cols = slice(g * POOL_GROUP_DIM, (g + 1) * POOL_GROUP_DIM)
            s = pext[:, cols]
            span = 1
            while span < win:
                s = s + _row_shift(s, span)
                span *= 2
            inv_count = jnp.where(t1 < win, inv_t1, 1.0 / win)
            diffs.append((s[POOL_HALO:, :] * inv_count - p[:, cols]).astype(BF16))
        pooled = []
        for pair in range(len(POOL_WINDOWS) // 2):
            cols = slice(pair * POOL_PAIR_DIM, (pair + 1) * POOL_PAIR_DIM)
            d_pair = jnp.concatenate(diffs[2 * pair:2 * pair + 2], axis=1)
            pooled.append((_dot(d_pair, pool_w_ref[pair]) + pool_b_ref[:, cols])
                          * pool_scale_ref[:, cols])
        b_out = jnp.concatenate(pooled, axis=1).astype(BF16)

        mix = _dot(jnp.concatenate([a_out, b_out], axis=1), w_out_ref[...])
        o_ref[r:r + chain_rows, :] = x_ref[r:r + chain_rows, :] + mix
    tail_ref[...] = tail


def _mixer_odd_kernel(x_ref, norm_ref, w_in_ref, conv_w_ref, conv_b_ref, w_out_ref,
                      o_ref, tail_ref, *, chain_rows):
    token_block = x_ref.shape[0]

    @pl.when(_sequence_block(token_block) == 0)
    def _():
        tail_ref[...] = jnp.zeros_like(tail_ref)

    def in_proj(r):
        hn = _rms_norm(x_ref[r:r + chain_rows, :], norm_ref[...]).astype(BF16)
        b_gate = _dot(hn, w_in_ref[:, :D_MODEL])
        q = _dot(hn, w_in_ref[:, D_MODEL:2 * D_MODEL]) * _dot(hn, w_in_ref[:, 2 * D_MODEL:])
        return b_gate, q

    chains = range(0, token_block, chain_rows)
    projected = [in_proj(r) for r in chains]
    tail = tail_ref[...]
    for r, (b_gate, q) in zip(chains, projected):
        qext = jnp.concatenate([tail, q], axis=0)
        tail = q[chain_rows - CONV_HALO:, :]
        y = conv_b_ref[...] + conv_w_ref[CONV_WIDTH - 1:CONV_WIDTH, :] * q
        for k in range(CONV_WIDTH - 1):
            shift = CONV_WIDTH - 1 - k
            y = y + conv_w_ref[k:k + 1, :] * _row_shift(qext, shift)[CONV_HALO:, :]
        o_ref[r:r + chain_rows, :] = (x_ref[r:r + chain_rows, :]
                                     + _dot((b_gate * y).astype(BF16), w_out_ref[...]))
    tail_ref[...] = tail


def _ffn_kernel(x_ref, norm_ref, w_gate_ref, w_up_ref, w_down_ref, *rest, chain_rows, final_norm):
    token_block = x_ref.shape[0]
    if final_norm:
        final_ref, o_ref = rest
    else:
        (o_ref,) = rest
    for r in range(0, token_block, chain_rows):
        x = x_ref[r:r + chain_rows, :]
        hn = _rms_norm(x, norm_ref[...]).astype(BF16)
        acc = x
        for start, size in FF_CHUNKS:
            gate = _dot(hn, w_gate_ref[:, start:start + size])
            up = _dot(hn, w_up_ref[:, start:start + size])
            act = (gate * (1.0 / (1.0 + jnp.exp(-gate))) * up).astype(BF16)
            acc = acc + _dot(act, w_down_ref[start:start + size, :])
        if final_norm:
            acc = _rms_norm(acc, final_ref[...])
        o_ref[r:r + chain_rows, :] = acc


def _resident(shape):
    nd = len(shape)
    return pl.BlockSpec(shape, lambda i: (0,) * nd, pipeline_mode=pl.Buffered(1))


def _token_call(body, name, x, params, tiling, scratch_shapes=()):
    tokens, d = x.shape
    token_block, chain_rows = tiling
    assert SEQ % token_block == 0 and token_block % chain_rows == 0 and chain_rows % SGU_BLOCK == 0
    tok_spec = pl.BlockSpec((token_block, d), lambda i: (i, 0))
    return pl.pallas_call(
        functools.partial(body, chain_rows=chain_rows),
        out_shape=jax.ShapeDtypeStruct((tokens, d), F32),
        grid=(tokens // token_block,),
        in_specs=[tok_spec] + [_resident(p.shape) for p in params],
        out_specs=tok_spec,
        scratch_shapes=list(scratch_shapes),
        compiler_params=pltpu.CompilerParams(
            dimension_semantics=("arbitrary",),
            vmem_limit_bytes=V7X_VMEM_LIMIT_BYTES,
        ),
        name=name,
    )(x, *params)


def _row(v):
    return v.reshape(1, -1).astype(F32)


def _pair_block_diagonal(w):
    g, c, _ = w.shape
    zero = jnp.zeros((g // 2, c, c), w.dtype)
    top = jnp.concatenate([w[0::2], zero], axis=2)
    bottom = jnp.concatenate([zero, w[1::2]], axis=2)
    return jnp.concatenate([top, bottom], axis=1)


def kernel(x, even_norm, even_w_in, even_sgu_ln_g, even_sgu_ln_b, even_sgu_ws, even_sgu_bs, even_pool_w, even_pool_b, even_pool_scale, even_w_out, odd_norm, odd_w_in, odd_conv_w, odd_conv_b, odd_w_out, ffn_norm, ffn_w_gate, ffn_w_up, ffn_w_down, final_norm):
    batch, seq, d = x.shape
    assert (seq, d) == (SEQ, D_MODEL)
    depth = ffn_norm.shape[0]
    xt = x.reshape(batch * seq, d)

    for layer in range(depth):
        i = layer // 2
        if layer % 2 == 0:
            bs_full = jnp.repeat(even_sgu_bs[i].astype(F32), SGU_HEAD_DIM, axis=1)
            params = (
                _row(even_norm[i]), even_w_in[i].astype(BF16),
                _row(even_sgu_ln_g[i]), _row(even_sgu_ln_b[i]),
                even_sgu_ws[i].astype(BF16), bs_full,
                _pair_block_diagonal(even_pool_w[i].astype(BF16)),
                _row(even_pool_b[i]), _row(even_pool_scale[i]),
                even_w_out[i].astype(BF16),
            )
            xt = _token_call(_mixer_even_kernel, f"mixer_even_{i}", xt, params, MIXER_TILING,
                             scratch_shapes=[pltpu.VMEM((POOL_HALO, POOL_DIM), F32)])
        else:
            params = (
                _row(odd_norm[i]), odd_w_in[i].astype(BF16),
                odd_conv_w[i].astype(F32), _row(odd_conv_b[i]), odd_w_out[i].astype(BF16),
            )
            xt = _token_call(_mixer_odd_kernel, f"mixer_odd_{i}", xt, params, MIXER_TILING,
                             scratch_shapes=[pltpu.VMEM((CONV_HALO, D_MODEL), F32)])

        last = layer == depth - 1
        params = (_row(ffn_norm[layer]), ffn_w_gate[layer].astype(BF16),
                  ffn_w_up[layer].astype(BF16), ffn_w_down[layer].astype(BF16))
        if last:
            params = params + (_row(final_norm),)
        xt = _token_call(functools.partial(_ffn_kernel, final_norm=last),
                         f"ffn_{layer}", xt, params, FFN_TILING)

    return xt.reshape(batch, seq, d)
```

```python
import functools
import math

import jax
import jax.numpy as jnp
from jax import lax
from jax.experimental import pallas as pl
from jax.experimental.pallas import tpu as pltpu

D_MODEL = 1024
SEQ = 4096
CHUNK = 64
SGU_BLOCK = 128
SGU_HEADS = 4
SGU_DIM = D_MODEL // 2
SGU_HEAD_DIM = SGU_DIM // SGU_HEADS
POOL_WINDOWS = (2, 4, 8, 16)
POOL_DIM = D_MODEL // 2
POOL_GROUP_DIM = POOL_DIM // len(POOL_WINDOWS)
POOL_PAIR_DIM = 2 * POOL_GROUP_DIM
POOL_HALO = 16
CONV_WIDTH = 3
CONV_HALO = 8
D_FF = 2816
EPS = 1e-6

MIXER_TILING = (2048, 1024)
FFN_TILING = (1024, 1024)
FF_CHUNKS = ((0, 768), (768, 768), (1536, 768), (2304, 512))
V7X_VMEM_LIMIT_BYTES = 60 * 1024 * 1024

BF16 = jnp.bfloat16
F32 = jnp.float32


def _inv_rms(x):
    return lax.rsqrt(jnp.mean(x * x, axis=-1, keepdims=True) + EPS)


def _rms_norm(x, g):
    return x * _inv_rms(x) * g


def _dot(a, b):
    return jnp.dot(a, b, preferred_element_type=F32)


def _gelu_exact(x):
    return 0.5 * x * (1.0 + lax.erf(x * math.sqrt(0.5)))


def _row_shift(x, k):
    return pltpu.roll(x, k, 0)


def _sequence_block(token_block):
    return pl.program_id(0) % (SEQ // token_block)


def _mixer_even_kernel(x_ref, norm_ref, w_in_ref, ln_g_ref, ln_b_ref, ws_ref, bs_ref,
                       pool_w_ref, pool_b_ref, pool_scale_ref, w_out_ref,
                       o_ref, tail_ref, *, chain_rows):
    token_block = x_ref.shape[0]
    seq_block = _sequence_block(token_block)

    @pl.when(seq_block == 0)
    def _():
        tail_ref[...] = jnp.zeros_like(tail_ref)

    row = lax.broadcasted_iota(jnp.int32, (SGU_BLOCK, SGU_BLOCK), 0) // CHUNK
    col = lax.broadcasted_iota(jnp.int32, (SGU_BLOCK, SGU_BLOCK), 1) // CHUNK
    w_heads = [jnp.where(col <= row, ws_ref[hd], jnp.zeros((), BF16)) for hd in range(SGU_HEADS)]
    n_blocks = chain_rows // SGU_BLOCK

    def in_proj(r):
        hn = _rms_norm(x_ref[r:r + chain_rows, :], norm_ref[...]).astype(BF16)
        return _dot(hn, w_in_ref[...])

    chains = range(0, token_block, chain_rows)
    projected = [in_proj(r) for r in chains]
    tail = tail_ref[...]
    for r, h in zip(chains, projected):

        u = _gelu_exact(h[:, :SGU_DIM])
        zv = _gelu_exact(h[:, SGU_DIM:2 * SGU_DIM])
        mu = jnp.mean(zv, axis=-1, keepdims=True)
        zc = zv - mu
        var = jnp.mean(zc * zc, axis=-1, keepdims=True)
        v = (zc * lax.rsqrt(var + EPS) * ln_g_ref[...] + ln_b_ref[...]).astype(BF16)

        mixed = []
        for hd in range(SGU_HEADS):
            cols = slice(hd * SGU_HEAD_DIM, (hd + 1) * SGU_HEAD_DIM)
            v_blocks = jnp.concatenate(
                [v[n * SGU_BLOCK:(n + 1) * SGU_BLOCK, cols] for n in range(n_blocks)], axis=1)
            mixed.append(_dot(w_heads[hd], v_blocks))
        vs = jnp.concatenate(
            [jnp.concatenate([m[:, n * SGU_HEAD_DIM:(n + 1) * SGU_HEAD_DIM] for m in mixed], axis=1)
             + bs_ref[...] for n in range(n_blocks)], axis=0)
        a_out = (u * vs).astype(BF16)

        p = h[:, 2 * SGU_DIM:]
        pext = jnp.concatenate([tail, p], axis=0)
        tail = p[chain_rows - POOL_HALO:, :]
        t1 = (lax.broadcasted_iota(jnp.int32, (chain_rows, POOL_GROUP_DIM), 0)
              + (seq_block * token_block + r + 1)).astype(F32)
        inv_t1 = 1.0 / t1
        diffs = []
        for g, win in enumerate(POOL_WINDOWS):
            cols = slice(g * POOL_GROUP_DIM, (g + 1) * POOL_GROUP_DIM)
            s = pext[:, cols]
            span = 1
            while span < win:
                s = s + _row_shift(s, span)
                span *= 2
            inv_count = jnp.where(t1 < win, inv_t1, 1.0 / win)
            diffs.append((s[POOL_HALO:, :] * inv_count - p[:, cols]).astype(BF16))
        pooled = []
        for pair in range(len(POOL_WINDOWS) // 2):
            cols = slice(pair * POOL_PAIR_DIM, (pair + 1) * POOL_PAIR_DIM)
            d_pair = jnp.concatenate(diffs[2 * pair:2 * pair + 2], axis=1)
            pooled.append((_dot(d_pair, pool_w_ref[pair]) + pool_b_ref[:, cols])
                          * pool_scale_ref[:, cols])
        b_out = jnp.concatenate(pooled, axis=1).astype(BF16)

        mix = _dot(jnp.concatenate([a_out, b_out], axis=1), w_out_ref[...])
        o_ref[r:r + chain_rows, :] = x_ref[r:r + chain_rows, :] + mix
    tail_ref[...] = tail


def _mixer_odd_kernel(x_ref, norm_ref, w_in_ref, conv_w_ref, conv_b_ref, w_out_ref,
                      o_ref, hist_ref, *, chain_rows):
    token_block = x_ref.shape[0]

    @pl.when(_sequence_block(token_block) == 0)
    def _():
        hist_ref[:CONV_HALO, :] = jnp.zeros((CONV_HALO, D_MODEL), F32)

    def in_proj(r):
        hn = _rms_norm(x_ref[r:r + chain_rows, :], norm_ref[...]).astype(BF16)
        b_gate = _dot(hn, w_in_ref[:, :D_MODEL])
        q = _dot(hn, w_in_ref[:, D_MODEL:2 * D_MODEL]) * _dot(hn, w_in_ref[:, 2 * D_MODEL:])
        return b_gate, q

    chains = range(0, token_block, chain_rows)
    projected = [in_proj(r) for r in chains]
    for r, (b_gate, q) in zip(chains, projected):
        hist_ref[CONV_HALO:, :] = q
        y = conv_b_ref[...] + conv_w_ref[CONV_WIDTH - 1:CONV_WIDTH, :] * q
        for k in range(CONV_WIDTH - 1):
            shift = CONV_WIDTH - 1 - k
            y = y + conv_w_ref[k:k + 1, :] * hist_ref[CONV_HALO - shift:CONV_HALO - shift + chain_rows, :]
        hist_ref[:CONV_HALO, :] = q[chain_rows - CONV_HALO:, :]
        o_ref[r:r + chain_rows, :] = (x_ref[r:r + chain_rows, :]
                                     + _dot((b_gate * y).astype(BF16), w_out_ref[...]))


def _ffn_kernel(x_ref, norm_ref, w_gate_ref, w_up_ref, w_down_ref, *rest, chain_rows, final_norm):
    token_block = x_ref.shape[0]
    if final_norm:
        final_ref, o_ref = rest
    else:
        (o_ref,) = rest
    for r in range(0, token_block, chain_rows):
        x = x_ref[r:r + chain_rows, :]
        xg = (x * norm_ref[...]).astype(BF16)
        inv_rms = _inv_rms(x)
        acc = x
        for start, size in FF_CHUNKS:
            gate = inv_rms * _dot(xg, w_gate_ref[:, start:start + size])
            up = inv_rms * _dot(xg, w_up_ref[:, start:start + size])
            act = (gate * (1.0 / (1.0 + jnp.exp(-gate))) * up).astype(BF16)
            acc = acc + _dot(act, w_down_ref[start:start + size, :])
        if final_norm:
            acc = _rms_norm(acc, final_ref[...])
        o_ref[r:r + chain_rows, :] = acc


def _resident(shape):
    nd = len(shape)
    return pl.BlockSpec(shape, lambda i: (0,) * nd, pipeline_mode=pl.Buffered(1))


def _token_call(body, name, x, params, tiling, scratch_shapes=()):
    tokens, d = x.shape
    token_block, chain_rows = tiling
    assert SEQ % token_block == 0 and token_block % chain_rows == 0 and chain_rows % SGU_BLOCK == 0
    tok_spec = pl.BlockSpec((token_block, d), lambda i: (i, 0))
    return pl.pallas_call(
        functools.partial(body, chain_rows=chain_rows),
        out_shape=jax.ShapeDtypeStruct((tokens, d), F32),
        grid=(tokens // token_block,),
        in_specs=[tok_spec] + [_resident(p.shape) for p in params],
        out_specs=tok_spec,
        scratch_shapes=list(scratch_shapes),
        compiler_params=pltpu.CompilerParams(
            dimension_semantics=("arbitrary",),
            vmem_limit_bytes=V7X_VMEM_LIMIT_BYTES,
        ),
        name=name,
    )(x, *params)


def _row(v):
    return v.reshape(1, -1).astype(F32)


def _pair_block_diagonal(w):
    g, c, _ = w.shape
    zero = jnp.zeros((g // 2, c, c), w.dtype)
    top = jnp.concatenate([w[0::2], zero], axis=2)
    bottom = jnp.concatenate([zero, w[1::2]], axis=2)
    return jnp.concatenate([top, bottom], axis=1)


def kernel(x, even_norm, even_w_in, even_sgu_ln_g, even_sgu_ln_b, even_sgu_ws, even_sgu_bs, even_pool_w, even_pool_b, even_pool_scale, even_w_out, odd_norm, odd_w_in, odd_conv_w, odd_conv_b, odd_w_out, ffn_norm, ffn_w_gate, ffn_w_up, ffn_w_down, final_norm):
    batch, seq, d = x.shape
    assert (seq, d) == (SEQ, D_MODEL)
    depth = ffn_norm.shape[0]
    xt = x.reshape(batch * seq, d)

    for layer in range(depth):
        i = layer // 2
        if layer % 2 == 0:
            bs_full = jnp.repeat(even_sgu_bs[i].astype(F32), SGU_HEAD_DIM, axis=1)
            params = (
                _row(even_norm[i]), even_w_in[i].astype(BF16),
                _row(even_sgu_ln_g[i]), _row(even_sgu_ln_b[i]),
                even_sgu_ws[i].astype(BF16), bs_full,
                _pair_block_diagonal(even_pool_w[i].astype(BF16)),
                _row(even_pool_b[i]), _row(even_pool_scale[i]),
                even_w_out[i].astype(BF16),
            )
            xt = _token_call(_mixer_even_kernel, f"mixer_even_{i}", xt, params, MIXER_TILING,
                             scratch_shapes=[pltpu.VMEM((POOL_HALO, POOL_DIM), F32)])
        else:
            params = (
                _row(odd_norm[i]), odd_w_in[i].astype(BF16),
                odd_conv_w[i].astype(F32), _row(odd_conv_b[i]), odd_w_out[i].astype(BF16),
            )
            xt = _token_call(_mixer_odd_kernel, f"mixer_odd_{i}", xt, params, MIXER_TILING,
                             scratch_shapes=[pltpu.VMEM((CONV_HALO + MIXER_TILING[1], D_MODEL), F32)])

        last = layer == depth - 1
        params = (_row(ffn_norm[layer]), ffn_w_gate[layer].astype(BF16),
                  ffn_w_up[layer].astype(BF16), ffn_w_down[layer].astype(BF16))
        if last:
            params = params + (_row(final_norm),)
        xt = _token_call(functools.partial(_ffn_kernel, final_norm=last),
                         f"ffn_{layer}", xt, params, FFN_TILING)

    return xt.reshape(batch, seq, d)
```

```python
import functools
import math

import jax
import jax.numpy as jnp
from jax import lax
from jax.experimental import pallas as pl
from jax.experimental.pallas import tpu as pltpu

D_MODEL = 1024
SEQ = 4096
CHUNK = 64
SGU_BLOCK = 128
SGU_HEADS = 4
SGU_DIM = D_MODEL // 2
SGU_HEAD_DIM = SGU_DIM // SGU_HEADS
POOL_WINDOWS = (2, 4, 8, 16)
POOL_DIM = D_MODEL // 2
POOL_GROUP_DIM = POOL_DIM // len(POOL_WINDOWS)
POOL_PAIR_DIM = 2 * POOL_GROUP_DIM
POOL_HALO = 16
CONV_WIDTH = 3
CONV_HALO = 8
D_FF = 2816
WEIGHT_CAST_ROWS = 64
EPS = 1e-6

TOKEN_BLOCK = 1024
CHAIN_ROWS = 512
FF_CHUNKS = ((0, 768), (768, 768), (1536, 768), (2304, 512))
V7X_VMEM_LIMIT_BYTES = 60 * 1024 * 1024

BF16 = jnp.bfloat16
F32 = jnp.float32


def _inv_rms(x):
    return lax.rsqrt(jnp.mean(x * x, axis=-1, keepdims=True) + EPS)


def _rms_norm(x, g):
    return x * _inv_rms(x) * g


def _dot(a, b):
    return jnp.dot(a, b, preferred_element_type=F32)


def _gelu_exact(x):
    return 0.5 * x * (1.0 + lax.erf(x * math.sqrt(0.5)))


def _row_shift(x, k):
    return pltpu.roll(x, k, 0)


def _sequence_block():
    return pl.program_id(0) % (SEQ // TOKEN_BLOCK)


def _mixer_even_chain(x, tail, first_row, norm_ref, w_in_ref, ln_g_ref, ln_b_ref, w_heads, bs_ref,
                      pool_w_ref, pool_b_ref, pool_scale_ref, w_out_ref):
    rows = x.shape[0]
    n_blocks = rows // SGU_BLOCK
    hn = _rms_norm(x, norm_ref[...]).astype(BF16)
    h = _dot(hn, w_in_ref[...])

    u = _gelu_exact(h[:, :SGU_DIM])
    zv = _gelu_exact(h[:, SGU_DIM:2 * SGU_DIM])
    mu = jnp.mean(zv, axis=-1, keepdims=True)
    zc = zv - mu
    var = jnp.mean(zc * zc, axis=-1, keepdims=True)
    v = (zc * lax.rsqrt(var + EPS) * ln_g_ref[...] + ln_b_ref[...]).astype(BF16)

    mixed = []
    for hd in range(SGU_HEADS):
        cols = slice(hd * SGU_HEAD_DIM, (hd + 1) * SGU_HEAD_DIM)
        v_blocks = jnp.concatenate(
            [v[n * SGU_BLOCK:(n + 1) * SGU_BLOCK, cols] for n in range(n_blocks)], axis=1)
        mixed.append(_dot(w_heads[hd], v_blocks))
    vs = jnp.concatenate(
        [jnp.concatenate([m[:, n * SGU_HEAD_DIM:(n + 1) * SGU_HEAD_DIM] for m in mixed], axis=1)
         + bs_ref[...] for n in range(n_blocks)], axis=0)
    a_out = (u * vs).astype(BF16)

    p = h[:, 2 * SGU_DIM:]
    pext = jnp.concatenate([tail, p], axis=0)
    t1 = (lax.broadcasted_iota(jnp.int32, (rows, POOL_GROUP_DIM), 0)
          + (first_row + 1)).astype(F32)
    inv_t1 = 1.0 / t1
    diffs = []
    for g, win in enumerate(POOL_WINDOWS):
        cols = slice(g * POOL_GROUP_DIM, (g + 1) * POOL_GROUP_DIM)
        s = pext[:, cols]
        span = 1
        while span < win:
            s = s + _row_shift(s, span)
            span *= 2
        inv_count = jnp.where(t1 < win, inv_t1, 1.0 / win)
        diffs.append((s[POOL_HALO:, :] * inv_count - p[:, cols]).astype(BF16))
    pooled = []
    for pair in range(len(POOL_WINDOWS) // 2):
        cols = slice(pair * POOL_PAIR_DIM, (pair + 1) * POOL_PAIR_DIM)
        d_pair = jnp.concatenate(diffs[2 * pair:2 * pair + 2], axis=1)
        pooled.append((_dot(d_pair, pool_w_ref[pair]) + pool_b_ref[:, cols])
                      * pool_scale_ref[:, cols])
    b_out = jnp.concatenate(pooled, axis=1).astype(BF16)

    mix = _dot(jnp.concatenate([a_out, b_out], axis=1), w_out_ref[...])
    return x + mix, p[rows - POOL_HALO:, :]


def _mixer_odd_chain(x, hist_ref, norm_ref, w_in_ref, conv_w_ref, conv_b_ref, w_out_ref):
    rows = x.shape[0]
    hn = _rms_norm(x, norm_ref[...]).astype(BF16)
    b_gate = _dot(hn, w_in_ref[:, :D_MODEL])
    q = _dot(hn, w_in_ref[:, D_MODEL:2 * D_MODEL]) * _dot(hn, w_in_ref[:, 2 * D_MODEL:])
    hist_ref[CONV_HALO:, :] = q
    y = conv_b_ref[...] + conv_w_ref[CONV_WIDTH - 1:CONV_WIDTH, :] * q
    for k in range(CONV_WIDTH - 1):
        lo = CONV_HALO - (CONV_WIDTH - 1 - k)
        y = y + conv_w_ref[k:k + 1, :] * hist_ref[lo:lo + rows, :]
    hist_ref[:CONV_HALO, :] = q[rows - CONV_HALO:, :]
    return x + _dot((b_gate * y).astype(BF16), w_out_ref[...])


def _ffn_chain(x, norm_ref, w_gate_ref, w_up_ref, w_down_ref):
    xg = (x * norm_ref[...]).astype(BF16)
    inv_rms = _inv_rms(x)
    acc = x
    for start, size in FF_CHUNKS:
        gate = inv_rms * _dot(xg, w_gate_ref[:, start:start + size])
        up = inv_rms * _dot(xg, w_up_ref[:, start:start + size])
        act = (gate * (1.0 / (1.0 + jnp.exp(-gate))) * up).astype(BF16)
        acc = acc + _dot(act, w_down_ref[start:start + size, :])
    return acc


def _layer_even_kernel(x_ref, norm_ref, w_in_ref, ln_g_ref, ln_b_ref, ws_ref, bs_ref,
                       pool_w_ref, pool_b_ref, pool_scale_ref, w_out_ref,
                       ffn_norm_ref, w_gate_ref, w_up_ref, w_down_ref, o_ref, tail_ref):
    seq_block = _sequence_block()

    @pl.when(seq_block == 0)
    def _():
        tail_ref[...] = jnp.zeros_like(tail_ref)

    row = lax.broadcasted_iota(jnp.int32, (SGU_BLOCK, SGU_BLOCK), 0) // CHUNK
    col = lax.broadcasted_iota(jnp.int32, (SGU_BLOCK, SGU_BLOCK), 1) // CHUNK
    w_heads = [jnp.where(col <= row, ws_ref[hd], jnp.zeros((), BF16)) for hd in range(SGU_HEADS)]

    tail = tail_ref[...]
    for r in range(0, TOKEN_BLOCK, CHAIN_ROWS):
        x, tail = _mixer_even_chain(
            x_ref[r:r + CHAIN_ROWS, :], tail, seq_block * TOKEN_BLOCK + r,
            norm_ref, w_in_ref, ln_g_ref, ln_b_ref, w_heads, bs_ref,
            pool_w_ref, pool_b_ref, pool_scale_ref, w_out_ref)
        o_ref[r:r + CHAIN_ROWS, :] = _ffn_chain(x, ffn_norm_ref, w_gate_ref, w_up_ref, w_down_ref)
    tail_ref[...] = tail


def _layer_odd_kernel(x_ref, norm_ref, w_in_ref, conv_w_ref, conv_b_ref, w_out_ref,
                      ffn_norm_ref, w_gate_ref, w_up_ref, w_down_ref, *rest, final_norm):
    if final_norm:
        final_ref, o_ref, hist_ref = rest
    else:
        o_ref, hist_ref = rest

    @pl.when(_sequence_block() == 0)
    def _():
        hist_ref[:CONV_HALO, :] = jnp.zeros((CONV_HALO, D_MODEL), F32)

    for r in range(0, TOKEN_BLOCK, CHAIN_ROWS):
        x = _mixer_odd_chain(x_ref[r:r + CHAIN_ROWS, :], hist_ref,
                             norm_ref, w_in_ref, conv_w_ref, conv_b_ref, w_out_ref)
        out = _ffn_chain(x, ffn_norm_ref, w_gate_ref, w_up_ref, w_down_ref)
        if final_norm:
            out = _rms_norm(out, final_ref[...])
        o_ref[r:r + CHAIN_ROWS, :] = out


def _resident(shape):
    nd = len(shape)
    return pl.BlockSpec(shape, lambda i: (0,) * nd, pipeline_mode=pl.Buffered(1))


def _with_weight_casts(layer_body, n_params, n_casts):
    def body(x_ref, *refs):
        params, cast_in = refs[:n_params], refs[n_params:n_params + n_casts]
        o_ref = refs[n_params + n_casts]
        cast_out = refs[n_params + n_casts + 1:n_params + 2 * n_casts + 1]
        scratch = refs[n_params + 2 * n_casts + 1:]
        for src_ref, dst_ref in zip(cast_in, cast_out):
            dst_ref[...] = src_ref[...].astype(BF16)
        layer_body(x_ref, *params, o_ref, *scratch)
    return body


def _weight_block_spec(w, steps):
    last_block = w.shape[0] // WEIGHT_CAST_ROWS - 1
    assert w.ndim == 2 and w.shape[0] % WEIGHT_CAST_ROWS == 0 and last_block < steps
    return pl.BlockSpec((WEIGHT_CAST_ROWS, w.shape[1]), lambda i: (jnp.minimum(i, last_block), 0))


def _layer_call(layer_body, name, x, params, scratch_shapes, next_weights=()):
    tokens, d = x.shape
    steps = tokens // TOKEN_BLOCK
    tok_spec = pl.BlockSpec((TOKEN_BLOCK, d), lambda i: (i, 0))
    outs = pl.pallas_call(
        _with_weight_casts(layer_body, len(params), len(next_weights)),
        out_shape=[jax.ShapeDtypeStruct((tokens, d), F32)]
        + [jax.ShapeDtypeStruct(w.shape, BF16) for w in next_weights],
        grid=(steps,),
        in_specs=[tok_spec] + [_resident(p.shape) for p in params]
        + [_weight_block_spec(w, steps) for w in next_weights],
        out_specs=[tok_spec] + [_weight_block_spec(w, steps) for w in next_weights],
        scratch_shapes=list(scratch_shapes),
        compiler_params=pltpu.CompilerParams(
            dimension_semantics=("arbitrary",),
            vmem_limit_bytes=V7X_VMEM_LIMIT_BYTES,
        ),
        name=name,
    )(x, *params, *next_weights)
    return outs[0], tuple(outs[1:])


def _row(v):
    return v.reshape(1, -1).astype(F32)


def _pair_block_diagonal(w):
    g, c, _ = w.shape
    zero = jnp.zeros((g // 2, c, c), w.dtype)
    top = jnp.concatenate([w[0::2], zero], axis=2)
    bottom = jnp.concatenate([zero, w[1::2]], axis=2)
    return jnp.concatenate([top, bottom], axis=1)


def kernel(x, even_norm, even_w_in, even_sgu_ln_g, even_sgu_ln_b, even_sgu_ws, even_sgu_bs, even_pool_w, even_pool_b, even_pool_scale, even_w_out, odd_norm, odd_w_in, odd_conv_w, odd_conv_b, odd_w_out, ffn_norm, ffn_w_gate, ffn_w_up, ffn_w_down, final_norm):
    batch, seq, d = x.shape
    assert (seq, d) == (SEQ, D_MODEL)
    assert SEQ % TOKEN_BLOCK == 0 and TOKEN_BLOCK % CHAIN_ROWS == 0 and CHAIN_ROWS % SGU_BLOCK == 0
    depth = ffn_norm.shape[0]
    xt = x.reshape(batch * seq, d)

    def matmul_weights(layer):
        i = layer // 2
        mixer = (even_w_in[i], even_w_out[i]) if layer % 2 == 0 else (odd_w_in[i], odd_w_out[i])
        return mixer + (ffn_w_gate[layer], ffn_w_up[layer], ffn_w_down[layer])

    weights = tuple(w.astype(BF16) for w in matmul_weights(0))
    for layer in range(depth):
        i = layer // 2
        last = layer == depth - 1
        next_weights = () if last else matmul_weights(layer + 1)
        w_in, w_out, w_gate, w_up, w_down = weights
        ffn_params = (_row(ffn_norm[layer]), w_gate, w_up, w_down)
        if layer % 2 == 0:
            assert not last, "the final norm is fused into the odd-layer call"
            bs_full = jnp.repeat(even_sgu_bs[i].astype(F32), SGU_HEAD_DIM, axis=1)
            params = (
                _row(even_norm[i]), w_in,
                _row(even_sgu_ln_g[i]), _row(even_sgu_ln_b[i]),
                even_sgu_ws[i].astype(BF16), bs_full,
                _pair_block_diagonal(even_pool_w[i].astype(BF16)),
                _row(even_pool_b[i]), _row(even_pool_scale[i]),
                w_out,
            ) + ffn_params
            xt, weights = _layer_call(
                _layer_even_kernel, f"layer_even_{i}", xt, params,
                [pltpu.VMEM((POOL_HALO, POOL_DIM), F32)], next_weights)
        else:
            params = (
                _row(odd_norm[i]), w_in,
                odd_conv_w[i].astype(F32), _row(odd_conv_b[i]), w_out,
            ) + ffn_params + ((_row(final_norm),) if last else ())
            xt, weights = _layer_call(
                functools.partial(_layer_odd_kernel, final_norm=last), f"layer_odd_{i}", xt, params,
                [pltpu.VMEM((CONV_HALO + CHAIN_ROWS, D_MODEL), F32)], next_weights)

    return xt.reshape(batch, seq, d)
```

```python
import functools
import math

import jax
import jax.numpy as jnp
from jax import lax
from jax.experimental import pallas as pl
from jax.experimental.pallas import tpu as pltpu

D_MODEL = 1024
SEQ = 4096
CHUNK = 64
SGU_BLOCK = 128
SGU_HEADS = 4
SGU_DIM = D_MODEL // 2
SGU_HEAD_DIM = SGU_DIM // SGU_HEADS
POOL_WINDOWS = (2, 4, 8, 16)
POOL_DIM = D_MODEL // 2
POOL_GROUP_DIM = POOL_DIM // len(POOL_WINDOWS)
POOL_PAIR_DIM = 2 * POOL_GROUP_DIM
POOL_HALO = 16
CONV_WIDTH = 3
CONV_HALO = 8
D_FF = 2816
WEIGHT_CAST_ROWS = 64
EPS = 1e-6

MIXER_TILING = (2048, 1024)
FFN_TILING = (1024, 1024)
FF_CHUNKS = ((0, 768), (768, 768), (1536, 768), (2304, 512))
V7X_VMEM_LIMIT_BYTES = 60 * 1024 * 1024

BF16 = jnp.bfloat16
F32 = jnp.float32


def _inv_rms(x):
    return lax.rsqrt(jnp.mean(x * x, axis=-1, keepdims=True) + EPS)


def _rms_norm(x, g):
    return x * _inv_rms(x) * g


def _dot(a, b):
    return jnp.dot(a, b, preferred_element_type=F32)


def _gelu_exact(x):
    return 0.5 * x * (1.0 + lax.erf(x * math.sqrt(0.5)))


def _row_shift(x, k):
    return pltpu.roll(x, k, 0)


def _sequence_block(token_block):
    return pl.program_id(0) % (SEQ // token_block)


def _mixer_even_kernel(x_ref, norm_ref, w_in_ref, ln_g_ref, ln_b_ref, ws_ref, bs_ref,
                       pool_w_ref, pool_b_ref, pool_scale_ref, w_out_ref,
                       o_ref, tail_ref, *, chain_rows):
    token_block = x_ref.shape[0]
    seq_block = _sequence_block(token_block)

    @pl.when(seq_block == 0)
    def _():
        tail_ref[...] = jnp.zeros_like(tail_ref)

    row = lax.broadcasted_iota(jnp.int32, (SGU_BLOCK, SGU_BLOCK), 0) // CHUNK
    col = lax.broadcasted_iota(jnp.int32, (SGU_BLOCK, SGU_BLOCK), 1) // CHUNK
    w_heads = [jnp.where(col <= row, ws_ref[hd], jnp.zeros((), BF16)) for hd in range(SGU_HEADS)]
    n_blocks = chain_rows // SGU_BLOCK

    def in_proj(r):
        hn = _rms_norm(x_ref[r:r + chain_rows, :], norm_ref[...]).astype(BF16)
        return _dot(hn, w_in_ref[...])

    chains = range(0, token_block, chain_rows)
    projected = [in_proj(r) for r in chains]
    tail = tail_ref[...]
    for r, h in zip(chains, projected):

        u = _gelu_exact(h[:, :SGU_DIM])
        zv = _gelu_exact(h[:, SGU_DIM:2 * SGU_DIM])
        mu = jnp.mean(zv, axis=-1, keepdims=True)
        zc = zv - mu
        var = jnp.mean(zc * zc, axis=-1, keepdims=True)
        v = (zc * lax.rsqrt(var + EPS) * ln_g_ref[...] + ln_b_ref[...]).astype(BF16)

        mixed = []
        for hd in range(SGU_HEADS):
            cols = slice(hd * SGU_HEAD_DIM, (hd + 1) * SGU_HEAD_DIM)
            v_blocks = jnp.concatenate(
                [v[n * SGU_BLOCK:(n + 1) * SGU_BLOCK, cols] for n in range(n_blocks)], axis=1)
            mixed.append(_dot(w_heads[hd], v_blocks))
        vs = jnp.concatenate(
            [jnp.concatenate([m[:, n * SGU_HEAD_DIM:(n + 1) * SGU_HEAD_DIM] for m in mixed], axis=1)
             + bs_ref[...] for n in range(n_blocks)], axis=0)
        a_out = (u * vs).astype(BF16)

        p = h[:, 2 * SGU_DIM:]
        pext = jnp.concatenate([tail, p], axis=0)
        tail = p[chain_rows - POOL_HALO:, :]
        t1 = (lax.broadcasted_iota(jnp.int32, (chain_rows, POOL_GROUP_DIM), 0)
              + (seq_block * token_block + r + 1)).astype(F32)
        inv_t1 = 1.0 / t1
        diffs = []
        for g, win in enumerate(POOL_WINDOWS):
            cols = slice(g * POOL_GROUP_DIM, (g + 1) * POOL_GROUP_DIM)
            s = pext[:, cols]
            span = 1
            while span < win:
                s = s + _row_shift(s, span)
                span *= 2
            inv_count = jnp.where(t1 < win, inv_t1, 1.0 / win)
            diffs.append((s[POOL_HALO:, :] * inv_count - p[:, cols]).astype(BF16))
        pooled = []
        for pair in range(len(POOL_WINDOWS) // 2):
            cols = slice(pair * POOL_PAIR_DIM, (pair + 1) * POOL_PAIR_DIM)
            d_pair = jnp.concatenate(diffs[2 * pair:2 * pair + 2], axis=1)
            pooled.append((_dot(d_pair, pool_w_ref[pair]) + pool_b_ref[:, cols])
                          * pool_scale_ref[:, cols])
        b_out = jnp.concatenate(pooled, axis=1).astype(BF16)

        mix = _dot(jnp.concatenate([a_out, b_out], axis=1), w_out_ref[...])
        o_ref[r:r + chain_rows, :] = x_ref[r:r + chain_rows, :] + mix
    tail_ref[...] = tail


def _mixer_odd_kernel(x_ref, norm_ref, w_in_ref, conv_w_ref, conv_b_ref, w_out_ref,
                      o_ref, hist_ref, *, chain_rows):
    token_block = x_ref.shape[0]

    @pl.when(_sequence_block(token_block) == 0)
    def _():
        hist_ref[:CONV_HALO, :] = jnp.zeros((CONV_HALO, D_MODEL), F32)

    def in_proj(r):
        hn = _rms_norm(x_ref[r:r + chain_rows, :], norm_ref[...]).astype(BF16)
        b_gate = _dot(hn, w_in_ref[:, :D_MODEL])
        q = _dot(hn, w_in_ref[:, D_MODEL:2 * D_MODEL]) * _dot(hn, w_in_ref[:, 2 * D_MODEL:])
        return b_gate, q

    chains = range(0, token_block, chain_rows)
    projected = [in_proj(r) for r in chains]
    for r, (b_gate, q) in zip(chains, projected):
        hist_ref[CONV_HALO:, :] = q
        y = conv_b_ref[...] + conv_w_ref[CONV_WIDTH - 1:CONV_WIDTH, :] * q
        for k in range(CONV_WIDTH - 1):
            shift = CONV_WIDTH - 1 - k
            y = y + conv_w_ref[k:k + 1, :] * hist_ref[CONV_HALO - shift:CONV_HALO - shift + chain_rows, :]
        hist_ref[:CONV_HALO, :] = q[chain_rows - CONV_HALO:, :]
        o_ref[r:r + chain_rows, :] = (x_ref[r:r + chain_rows, :]
                                     + _dot((b_gate * y).astype(BF16), w_out_ref[...]))


def _ffn_kernel(x_ref, norm_ref, w_gate_ref, w_up_ref, w_down_ref, *rest, chain_rows, final_norm):
    token_block = x_ref.shape[0]
    if final_norm:
        final_ref, o_ref = rest
    else:
        (o_ref,) = rest
    for r in range(0, token_block, chain_rows):
        x = x_ref[r:r + chain_rows, :]
        xg = (x * norm_ref[...]).astype(BF16)
        inv_rms = _inv_rms(x)
        acc = x
        for start, size in FF_CHUNKS:
            gate = inv_rms * _dot(xg, w_gate_ref[:, start:start + size])
            up = inv_rms * _dot(xg, w_up_ref[:, start:start + size])
            act = (gate * (1.0 / (1.0 + jnp.exp(-gate))) * up).astype(BF16)
            acc = acc + _dot(act, w_down_ref[start:start + size, :])
        if final_norm:
            acc = _rms_norm(acc, final_ref[...])
        o_ref[r:r + chain_rows, :] = acc


def _resident(shape):
    nd = len(shape)
    return pl.BlockSpec(shape, lambda i: (0,) * nd, pipeline_mode=pl.Buffered(1))


def _with_weight_casts(body, n_params, cast_blocks):
    n_casts = len(cast_blocks)

    def with_casts(x_ref, *refs, **static):
        params, cast_in = refs[:n_params], refs[n_params:n_params + n_casts]
        o_ref = refs[n_params + n_casts]
        cast_out = refs[n_params + n_casts + 1:n_params + 2 * n_casts + 1]
        scratch = refs[n_params + 2 * n_casts + 1:]
        for src_ref, dst_ref, n_blocks in zip(cast_in, cast_out, cast_blocks):
            @pl.when(pl.program_id(0) < n_blocks)
            def _():
                dst_ref[...] = src_ref[...].astype(BF16)
        body(x_ref, *params, o_ref, *scratch, **static)
    return with_casts


def _cast_blocks(stacked, steps):
    rows = stacked.shape[1]
    assert rows % WEIGHT_CAST_ROWS == 0 and rows // WEIGHT_CAST_ROWS <= steps
    return rows // WEIGHT_CAST_ROWS


def _weight_block_specs(stacked, index, steps):
    cols = stacked.shape[2]
    last_block = _cast_blocks(stacked, steps) - 1
    return (pl.BlockSpec((None, WEIGHT_CAST_ROWS, cols),
                         lambda i: (index, jnp.minimum(i, last_block), 0)),
            pl.BlockSpec((WEIGHT_CAST_ROWS, cols), lambda i: (jnp.minimum(i, last_block), 0)))


def _token_call(body, name, x, params, tiling, scratch_shapes=(), cast_weights=()):
    tokens, d = x.shape
    token_block, chain_rows = tiling
    assert SEQ % token_block == 0 and token_block % chain_rows == 0 and chain_rows % SGU_BLOCK == 0
    steps = tokens // token_block
    tok_spec = pl.BlockSpec((token_block, d), lambda i: (i, 0))
    cast_specs = [_weight_block_specs(w, index, steps) for w, index in cast_weights]
    outs = pl.pallas_call(
        functools.partial(
            _with_weight_casts(body, len(params), [_cast_blocks(w, steps) for w, _ in cast_weights]),
            chain_rows=chain_rows),
        out_shape=[jax.ShapeDtypeStruct((tokens, d), F32)]
        + [jax.ShapeDtypeStruct(w.shape[1:], BF16) for w, _ in cast_weights],
        grid=(steps,),
        in_specs=[tok_spec] + [_resident(p.shape) for p in params] + [s[0] for s in cast_specs],
        out_specs=[tok_spec] + [s[1] for s in cast_specs],
        scratch_shapes=list(scratch_shapes),
        compiler_params=pltpu.CompilerParams(
            dimension_semantics=("arbitrary",),
            vmem_limit_bytes=V7X_VMEM_LIMIT_BYTES,
        ),
        name=name,
    )(x, *params, *[w for w, _ in cast_weights])
    return outs[0], tuple(outs[1:])


def _row(v):
    return v.reshape(1, -1).astype(F32)


def _pair_block_diagonal(w):
    g, c, _ = w.shape
    zero = jnp.zeros((g // 2, c, c), w.dtype)
    top = jnp.concatenate([w[0::2], zero], axis=2)
    bottom = jnp.concatenate([zero, w[1::2]], axis=2)
    return jnp.concatenate([top, bottom], axis=1)


def kernel(x, even_norm, even_w_in, even_sgu_ln_g, even_sgu_ln_b, even_sgu_ws, even_sgu_bs, even_pool_w, even_pool_b, even_pool_scale, even_w_out, odd_norm, odd_w_in, odd_conv_w, odd_conv_b, odd_w_out, ffn_norm, ffn_w_gate, ffn_w_up, ffn_w_down, final_norm):
    batch, seq, d = x.shape
    assert (seq, d) == (SEQ, D_MODEL)
    depth = ffn_norm.shape[0]
    xt = x.reshape(batch * seq, d)

    def stacked_matmul_weights(layer):
        i = layer // 2
        mixer = (even_w_in, even_w_out) if layer % 2 == 0 else (odd_w_in, odd_w_out)
        return tuple((w, i) for w in mixer) + tuple(
            (w, layer) for w in (ffn_w_gate, ffn_w_up, ffn_w_down))

    weights = tuple(w[index].astype(BF16) for w, index in stacked_matmul_weights(0))
    for layer in range(depth):
        i = layer // 2
        w_in, w_out, w_gate, w_up, w_down = weights
        if layer % 2 == 0:
            bs_full = jnp.repeat(even_sgu_bs[i].astype(F32), SGU_HEAD_DIM, axis=1)
            params = (
                _row(even_norm[i]), w_in,
                _row(even_sgu_ln_g[i]), _row(even_sgu_ln_b[i]),
                even_sgu_ws[i].astype(BF16), bs_full,
                _pair_block_diagonal(even_pool_w[i].astype(BF16)),
                _row(even_pool_b[i]), _row(even_pool_scale[i]),
                w_out,
            )
            xt, _ = _token_call(_mixer_even_kernel, f"mixer_even_{i}", xt, params, MIXER_TILING,
                                scratch_shapes=[pltpu.VMEM((POOL_HALO, POOL_DIM), F32)])
        else:
            params = (
                _row(odd_norm[i]), w_in,
                odd_conv_w[i].astype(F32), _row(odd_conv_b[i]), w_out,
            )
            xt, _ = _token_call(
                _mixer_odd_kernel, f"mixer_odd_{i}", xt, params, MIXER_TILING,
                scratch_shapes=[pltpu.VMEM((CONV_HALO + MIXER_TILING[1], D_MODEL), F32)])

        last = layer == depth - 1
        params = (_row(ffn_norm[layer]), w_gate, w_up, w_down)
        if last:
            params = params + (_row(final_norm),)
        xt, weights = _token_call(
            functools.partial(_ffn_kernel, final_norm=last), f"ffn_{layer}", xt, params, FFN_TILING,
            cast_weights=() if last else stacked_matmul_weights(layer + 1))

    return xt.reshape(batch, seq, d)
```

```python
import functools
import math

import jax
import jax.numpy as jnp
from jax import lax
from jax.experimental import pallas as pl
from jax.experimental.pallas import tpu as pltpu

D_MODEL = 1024
SEQ = 4096
CHUNK = 64
SGU_BLOCK = 128
SGU_HEADS = 4
SGU_DIM = D_MODEL // 2
SGU_HEAD_DIM = SGU_DIM // SGU_HEADS
POOL_WINDOWS = (2, 4, 8, 16)
POOL_DIM = D_MODEL // 2
POOL_GROUP_DIM = POOL_DIM // len(POOL_WINDOWS)
POOL_PAIR_DIM = 2 * POOL_GROUP_DIM
POOL_HALO = 16
CONV_WIDTH = 3
CONV_HALO = 8
D_FF = 2816
WEIGHT_CAST_ROWS = 64
EPS = 1e-6

MIXER_EVEN_CHAINS = (512, 512, 512, 512)
MIXER_ODD_CHAINS = (1024, 1024)
FFN_CHAINS = (1024,)
LAST_FFN_CHAINS = (1024, 1024)
FF_CHUNKS = ((0, 768), (768, 768), (1536, 768), (2304, 512))
V7X_VMEM_LIMIT_BYTES = 60 * 1024 * 1024

BF16 = jnp.bfloat16
F32 = jnp.float32


def _inv_rms(x):
    return lax.rsqrt(jnp.mean(x * x, axis=-1, keepdims=True) + EPS)


def _rms_norm(x, g):
    return x * _inv_rms(x) * g


def _dot(a, b):
    return jnp.dot(a, b, preferred_element_type=F32)


def _gelu_exact(x):
    return 0.5 * x * (1.0 + lax.erf(x * math.sqrt(0.5)))


def _row_shift(x, k):
    return pltpu.roll(x, k, 0)


def _chain_spans(chains):
    spans, first = [], 0
    for rows in chains:
        spans.append((first, rows))
        first += rows
    return spans


def _sequence_block(token_block):
    return pl.program_id(0) % (SEQ // token_block)


def _mixer_even_kernel(x_ref, norm_ref, w_in_ref, ln_g_ref, ln_b_ref, ws_ref, bs_ref,
                       pool_w_ref, pool_b_ref, pool_scale_ref, w_out_ref,
                       o_ref, tail_ref, *, chains):
    token_block = x_ref.shape[0]
    seq_block = _sequence_block(token_block)

    @pl.when(seq_block == 0)
    def _():
        tail_ref[...] = jnp.zeros_like(tail_ref)

    row = lax.broadcasted_iota(jnp.int32, (SGU_BLOCK, SGU_BLOCK), 0) // CHUNK
    col = lax.broadcasted_iota(jnp.int32, (SGU_BLOCK, SGU_BLOCK), 1) // CHUNK
    w_heads = [jnp.where(col <= row, ws_ref[hd], jnp.zeros((), BF16)) for hd in range(SGU_HEADS)]

    def in_proj(r, rows):
        hn = _rms_norm(x_ref[r:r + rows, :], norm_ref[...]).astype(BF16)
        return _dot(hn, w_in_ref[...])

    spans = _chain_spans(chains)
    projected = [in_proj(r, rows) for r, rows in spans]
    tail = tail_ref[...]
    for (r, rows), h in zip(spans, projected):
        n_blocks = rows // SGU_BLOCK

        u = _gelu_exact(h[:, :SGU_DIM])
        zv = _gelu_exact(h[:, SGU_DIM:2 * SGU_DIM])
        mu = jnp.mean(zv, axis=-1, keepdims=True)
        zc = zv - mu
        var = jnp.mean(zc * zc, axis=-1, keepdims=True)
        v = (zc * lax.rsqrt(var + EPS) * ln_g_ref[...] + ln_b_ref[...]).astype(BF16)

        mixed = []
        for hd in range(SGU_HEADS):
            cols = slice(hd * SGU_HEAD_DIM, (hd + 1) * SGU_HEAD_DIM)
            v_blocks = jnp.concatenate(
                [v[n * SGU_BLOCK:(n + 1) * SGU_BLOCK, cols] for n in range(n_blocks)], axis=1)
            mixed.append(_dot(w_heads[hd], v_blocks))
        vs = jnp.concatenate(
            [jnp.concatenate([m[:, n * SGU_HEAD_DIM:(n + 1) * SGU_HEAD_DIM] for m in mixed], axis=1)
             + bs_ref[...] for n in range(n_blocks)], axis=0)
        a_out = (u * vs).astype(BF16)

        p = h[:, 2 * SGU_DIM:]
        pext = jnp.concatenate([tail, p], axis=0)
        tail = p[rows - POOL_HALO:, :]
        t1 = (lax.broadcasted_iota(jnp.int32, (rows, POOL_GROUP_DIM), 0)
              + (seq_block * token_block + r + 1)).astype(F32)
        inv_t1 = 1.0 / t1
        diffs = []
        for g, win in enumerate(POOL_WINDOWS):
            cols = slice(g * POOL_GROUP_DIM, (g + 1) * POOL_GROUP_DIM)
            s = pext[:, cols]
            span = 1
            while span < win:
                s = s + _row_shift(s, span)
                span *= 2
            inv_count = jnp.where(t1 < win, inv_t1, 1.0 / win)
            diffs.append((s[POOL_HALO:, :] * inv_count - p[:, cols]).astype(BF16))
        pooled = []
        for pair in range(len(POOL_WINDOWS) // 2):
            cols = slice(pair * POOL_PAIR_DIM, (pair + 1) * POOL_PAIR_DIM)
            d_pair = jnp.concatenate(diffs[2 * pair:2 * pair + 2], axis=1)
            pooled.append((_dot(d_pair, pool_w_ref[pair]) + pool_b_ref[:, cols])
                          * pool_scale_ref[:, cols])
        b_out = jnp.concatenate(pooled, axis=1).astype(BF16)

        mix = _dot(jnp.concatenate([a_out, b_out], axis=1), w_out_ref[...])
        o_ref[r:r + rows, :] = x_ref[r:r + rows, :] + mix
    tail_ref[...] = tail


def _mixer_odd_kernel(x_ref, norm_ref, w_in_ref, conv_w_ref, conv_b_ref, w_out_ref,
                      o_ref, hist_ref, *, chains):
    token_block = x_ref.shape[0]

    @pl.when(_sequence_block(token_block) == 0)
    def _():
        hist_ref[:CONV_HALO, :] = jnp.zeros((CONV_HALO, D_MODEL), F32)

    def in_proj(r, rows):
        hn = _rms_norm(x_ref[r:r + rows, :], norm_ref[...]).astype(BF16)
        b_gate = _dot(hn, w_in_ref[:, :D_MODEL])
        q = _dot(hn, w_in_ref[:, D_MODEL:2 * D_MODEL]) * _dot(hn, w_in_ref[:, 2 * D_MODEL:])
        return b_gate, q

    spans = _chain_spans(chains)
    projected = [in_proj(r, rows) for r, rows in spans]
    for (r, rows), (b_gate, q) in zip(spans, projected):
        hist_ref[CONV_HALO:CONV_HALO + rows, :] = q
        y = conv_b_ref[...] + conv_w_ref[CONV_WIDTH - 1:CONV_WIDTH, :] * q
        for k in range(CONV_WIDTH - 1):
            shift = CONV_WIDTH - 1 - k
            y = y + conv_w_ref[k:k + 1, :] * hist_ref[CONV_HALO - shift:CONV_HALO - shift + rows, :]
        hist_ref[:CONV_HALO, :] = q[rows - CONV_HALO:, :]
        o_ref[r:r + rows, :] = (x_ref[r:r + rows, :]
                                + _dot((b_gate * y).astype(BF16), w_out_ref[...]))


def _ffn_kernel(x_ref, norm_ref, w_gate_ref, w_up_ref, w_down_ref, *rest, chains, final_norm):
    if final_norm:
        final_ref, o_ref = rest
    else:
        (o_ref,) = rest
    for r, rows in _chain_spans(chains):
        x = x_ref[r:r + rows, :]
        xg = (x * norm_ref[...]).astype(BF16)
        inv_rms = _inv_rms(x)
        acc = x
        for start, size in FF_CHUNKS:
            gate = inv_rms * _dot(xg, w_gate_ref[:, start:start + size])
            up = inv_rms * _dot(xg, w_up_ref[:, start:start + size])
            act = (gate * (1.0 / (1.0 + jnp.exp(-gate))) * up).astype(BF16)
            acc = acc + _dot(act, w_down_ref[start:start + size, :])
        if final_norm:
            acc = _rms_norm(acc, final_ref[...])
        o_ref[r:r + rows, :] = acc


def _resident(shape):
    nd = len(shape)
    return pl.BlockSpec(shape, lambda i: (0,) * nd, pipeline_mode=pl.Buffered(1))


def _with_weight_casts(body, n_params, cast_blocks):
    n_casts = len(cast_blocks)

    def with_casts(x_ref, *refs, **static):
        params, cast_in = refs[:n_params], refs[n_params:n_params + n_casts]
        o_ref = refs[n_params + n_casts]
        cast_out = refs[n_params + n_casts + 1:n_params + 2 * n_casts + 1]
        scratch = refs[n_params + 2 * n_casts + 1:]
        for src_ref, dst_ref, n_blocks in zip(cast_in, cast_out, cast_blocks):
            @pl.when(pl.program_id(0) < n_blocks)
            def _():
                dst_ref[...] = src_ref[...].astype(BF16)
        body(x_ref, *params, o_ref, *scratch, **static)
    return with_casts


def _cast_block_rows(stacked, steps):
    rows = stacked.shape[1]
    block_rows = WEIGHT_CAST_ROWS
    while rows // block_rows > steps:
        block_rows *= 2
    assert rows % block_rows == 0
    return block_rows


def _cast_blocks(stacked, steps):
    return stacked.shape[1] // _cast_block_rows(stacked, steps)


def _weight_block_specs(stacked, index, steps):
    cols = stacked.shape[2]
    block_rows = _cast_block_rows(stacked, steps)
    last_block = _cast_blocks(stacked, steps) - 1
    return (pl.BlockSpec((None, block_rows, cols),
                         lambda i: (index, jnp.minimum(i, last_block), 0)),
            pl.BlockSpec((block_rows, cols), lambda i: (jnp.minimum(i, last_block), 0)))


def _token_call(body, name, x, params, chains, scratch_shapes=(), cast_weights=()):
    tokens, d = x.shape
    token_block = sum(chains)
    assert SEQ % token_block == 0 and all(rows % SGU_BLOCK == 0 for rows in chains)
    steps = tokens // token_block
    tok_spec = pl.BlockSpec((token_block, d), lambda i: (i, 0))
    cast_specs = [_weight_block_specs(w, index, steps) for w, index in cast_weights]
    outs = pl.pallas_call(
        functools.partial(
            _with_weight_casts(body, len(params), [_cast_blocks(w, steps) for w, _ in cast_weights]),
            chains=chains),
        out_shape=[jax.ShapeDtypeStruct((tokens, d), F32)]
        + [jax.ShapeDtypeStruct(w.shape[1:], BF16) for w, _ in cast_weights],
        grid=(steps,),
        in_specs=[tok_spec] + [_resident(p.shape) for p in params] + [s[0] for s in cast_specs],
        out_specs=[tok_spec] + [s[1] for s in cast_specs],
        scratch_shapes=list(scratch_shapes),
        compiler_params=pltpu.CompilerParams(
            dimension_semantics=("arbitrary",),
            vmem_limit_bytes=V7X_VMEM_LIMIT_BYTES,
        ),
        name=name,
    )(x, *params, *[w for w, _ in cast_weights])
    return outs[0], tuple(outs[1:])


def _row(v):
    return v.reshape(1, -1).astype(F32)


def _pair_block_diagonal(w):
    g, c, _ = w.shape
    zero = jnp.zeros((g // 2, c, c), w.dtype)
    top = jnp.concatenate([w[0::2], zero], axis=2)
    bottom = jnp.concatenate([zero, w[1::2]], axis=2)
    return jnp.concatenate([top, bottom], axis=1)


def kernel(x, even_norm, even_w_in, even_sgu_ln_g, even_sgu_ln_b, even_sgu_ws, even_sgu_bs, even_pool_w, even_pool_b, even_pool_scale, even_w_out, odd_norm, odd_w_in, odd_conv_w, odd_conv_b, odd_w_out, ffn_norm, ffn_w_gate, ffn_w_up, ffn_w_down, final_norm):
    batch, seq, d = x.shape
    assert (seq, d) == (SEQ, D_MODEL)
    depth = ffn_norm.shape[0]
    xt = x.reshape(batch * seq, d)

    def stacked_mixer_weights(layer):
        mixer = (even_w_in, even_w_out) if layer % 2 == 0 else (odd_w_in, odd_w_out)
        return tuple((w, layer // 2) for w in mixer)

    def stacked_ffn_weights(layer):
        return tuple((w, layer) for w in (ffn_w_gate, ffn_w_up, ffn_w_down))

    w_in, w_out = (w[index].astype(BF16) for w, index in stacked_mixer_weights(0))
    ffn_weights = None
    for layer in range(depth):
        i = layer // 2
        mixer_casts = stacked_ffn_weights(layer) if ffn_weights is None else ()
        if layer % 2 == 0:
            bs_full = jnp.repeat(even_sgu_bs[i].astype(F32), SGU_HEAD_DIM, axis=1)
            params = (
                _row(even_norm[i]), w_in,
                _row(even_sgu_ln_g[i]), _row(even_sgu_ln_b[i]),
                even_sgu_ws[i].astype(BF16), bs_full,
                _pair_block_diagonal(even_pool_w[i].astype(BF16)),
                _row(even_pool_b[i]), _row(even_pool_scale[i]),
                w_out,
            )
            xt, cast = _token_call(
                _mixer_even_kernel, f"mixer_even_{i}", xt, params, MIXER_EVEN_CHAINS,
                scratch_shapes=[pltpu.VMEM((POOL_HALO, POOL_DIM), F32)], cast_weights=mixer_casts)
        else:
            params = (
                _row(odd_norm[i]), w_in,
                odd_conv_w[i].astype(F32), _row(odd_conv_b[i]), w_out,
            )
            xt, cast = _token_call(
                _mixer_odd_kernel, f"mixer_odd_{i}", xt, params, MIXER_ODD_CHAINS,
                scratch_shapes=[pltpu.VMEM((CONV_HALO + max(MIXER_ODD_CHAINS), D_MODEL), F32)],
                cast_weights=mixer_casts)
        if mixer_casts:
            ffn_weights = cast

        last = layer == depth - 1
        params = (_row(ffn_norm[layer]),) + ffn_weights
        if last:
            params = params + (_row(final_norm),)
        next_casts = () if last else stacked_mixer_weights(layer + 1) + stacked_ffn_weights(layer + 1)
        xt, cast = _token_call(
            functools.partial(_ffn_kernel, final_norm=last), f"ffn_{layer}", xt, params,
            LAST_FFN_CHAINS if last else FFN_CHAINS, cast_weights=next_casts)
        if not last:
            w_in, w_out, ffn_weights = cast[0], cast[1], cast[2:]

    return xt.reshape(batch, seq, d)
```
